```python
import math
import jax, jax.numpy as jnp
from jax import lax
import numpy as np

D_MODEL = 4096
BATCH = 4
SEQ = 2048
DEPTH = 2
DEC_BATCH = 32
DEC_SEQ = 1
PAST_LEN = 16384
PAGE_SIZE = 128

SSM_W = D_MODEL // 4
ATT_W = D_MODEL // 2
GLA_W = D_MODEL // 4
SSM_GROUP = 16
SSM_GROUPS = SSM_W // SSM_GROUP
SSM_STATE = 64
HEAD_DIM = 128
ATT_HEADS = ATT_W // HEAD_DIM
KV_HEADS = max(1, ATT_HEADS // 4)
KV_REP = ATT_HEADS // KV_HEADS
WINDOW = 128
GLA_HEADS = 4
GLA_DV = GLA_W // GLA_HEADS
GLA_DK = GLA_DV // 2
GLA_RANK = 16
GLA_TAU = 16.0
GLA_CHUNK = 64
IN_SPLITS = (SSM_W, ATT_W, KV_HEADS * HEAD_DIM, KV_HEADS * HEAD_DIM,
             GLA_HEADS * GLA_DK, GLA_HEADS * GLA_DK, GLA_W, GLA_W, GLA_RANK)
IN_COLS = SSM_W + ATT_W + 2 * KV_HEADS * HEAD_DIM + 2 * GLA_HEADS * GLA_DK + 2 * GLA_W + GLA_RANK
MOE_GROUPS = 4
EXPERTS_PER_GROUP = 8
N_EXPERTS = MOE_GROUPS * EXPERTS_PER_GROUP
TOP_K = 2
D_EXPERT = D_MODEL // 4
MOE_BLOCK = 128
RMS_EPS = 1e-6

kernel_name = 'hymba_s5_swa_gla_hmoe_step'


def rmsnorm(x, g):
    xf = x.astype(jnp.float32)
    y = xf * lax.rsqrt(jnp.mean(xf * xf, axis=-1, keepdims=True) + RMS_EPS)
    return (y * g.astype(jnp.float32)).astype(x.dtype)


def _split_cols(z):
    cuts = np.cumsum(np.array(IN_SPLITS))[:-1].tolist()
    return jnp.split(z, cuts, axis=-1)


def _cplx_combine(e1, e2):
    ar1, ai1, br1, bi1 = e1
    ar2, ai2, br2, bi2 = e2
    return (ar2 * ar1 - ai2 * ai1,
            ar2 * ai1 + ai2 * ar1,
            ar2 * br1 - ai2 * bi1 + br2,
            ar2 * bi1 + ai2 * br1 + bi2)


def s5_mixer(u, s0_re, s0_im, lam_re, lam_im, log_dt, b_re, b_im, c_re, c_im, d_skip):
    f32 = jnp.float32
    bsz, seq = u.shape[:2]
    uf = u.astype(f32).reshape(bsz, seq, SSM_GROUPS, SSM_GROUP)
    lr, li = lam_re.astype(f32), lam_im.astype(f32)
    dt = jnp.exp(log_dt.astype(f32))[:, None]
    mag = jnp.exp(lr * dt)
    ar, ai = mag * jnp.cos(li * dt), mag * jnp.sin(li * dt)
    den = lr * lr + li * li
    zr, zi = ar - 1.0, ai
    cr = (zr * lr + zi * li) / den
    ci = (zi * lr - zr * li) / den
    br, bi = b_re.astype(f32), b_im.astype(f32)
    bbr = cr[..., None] * br - ci[..., None] * bi
    bbi = cr[..., None] * bi + ci[..., None] * br
    xr = jnp.einsum('blgc,gpc->blgp', uf, bbr)
    xi = jnp.einsum('blgc,gpc->blgp', uf, bbi)
    s0r, s0i = s0_re.astype(f32), s0_im.astype(f32)
    xr = xr.at[:, 0].add(ar * s0r - ai * s0i)
    xi = xi.at[:, 0].add(ar * s0i + ai * s0r)
    _, _, hr, hi = lax.associative_scan(
        _cplx_combine,
        (jnp.broadcast_to(ar, xr.shape), jnp.broadcast_to(ai, xi.shape), xr, xi),
        axis=1)
    y = (jnp.einsum('blgp,gcp->blgc', hr, c_re.astype(f32))
         - jnp.einsum('blgp,gcp->blgc', hi, c_im.astype(f32))
         + d_skip.astype(f32).reshape(SSM_GROUPS, SSM_GROUP) * uf)
    return y.reshape(bsz, seq, SSM_W), hr[:, -1], hi[:, -1]


def _window_attend(q, k, v, qpos, kpos, sinks):
    f32 = jnp.float32
    s = jnp.einsum('bnqgrd,bnkgd->bngrqk', q.astype(f32), k.astype(f32)) * (HEAD_DIM ** -0.5)
    rel = qpos[:, :, None] - kpos[:, None, :]
    valid = (rel >= 0) & (rel < WINDOW) & (kpos[:, None, :] >= 0)
    s = jnp.where(valid[None, :, None, None], s, jnp.float32(-1e30))
    sink = jnp.broadcast_to(sinks.astype(f32).reshape(1, 1, KV_HEADS, KV_REP, 1, 1), s.shape[:-1] + (1,))
    p = jax.nn.softmax(jnp.concatenate([s, sink], axis=-1), axis=-1)[..., :-1]
    return jnp.einsum('bngrqk,bnkgd->bnqgrd', p, v.astype(f32))


def swa_prompt(q, k, v, sinks):
    bsz, seq = q.shape[:2]
    nb = seq // WINDOW
    qb = q.reshape(bsz, nb, WINDOW, KV_HEADS, KV_REP, HEAD_DIM)
    kb = k.reshape(bsz, nb, WINDOW, KV_HEADS, HEAD_DIM)
    vb = v.reshape(bsz, nb, WINDOW, KV_HEADS, HEAD_DIM)
    kk = jnp.concatenate([jnp.concatenate([jnp.zeros_like(kb[:, :1]), kb[:, :-1]], axis=1), kb], axis=2)
    vv = jnp.concatenate([jnp.concatenate([jnp.zeros_like(vb[:, :1]), vb[:, :-1]], axis=1), vb], axis=2)
    qpos = jnp.arange(seq, dtype=jnp.int32).reshape(nb, WINDOW)
    kpos = jnp.concatenate([qpos - WINDOW, qpos], axis=1)
    o = _window_attend(qb, kk, vv, qpos, kpos, sinks)
    return o.reshape(bsz, seq, ATT_W)


def swa_sample(q, k, v, buf_k, buf_v, sinks):
    bsz, seq = q.shape[:2]
    wb = buf_k.shape[1]
    kk = jnp.concatenate([buf_k.astype(k.dtype), k], axis=1)
    vv = jnp.concatenate([buf_v.astype(v.dtype), v], axis=1)
    qpos = (PAST_LEN + jnp.arange(seq, dtype=jnp.int32))[None]
    kpos = (PAST_LEN - wb + jnp.arange(wb + seq, dtype=jnp.int32))[None]
    o = _window_attend(q.reshape(bsz, 1, seq, KV_HEADS, KV_REP, HEAD_DIM), kk[:, None], vv[:, None], qpos, kpos, sinks)
    return o.reshape(bsz, seq, ATT_W), kk[:, -wb:], vv[:, -wb:]


def gla_chunked(q, k, v, log_a, s0):
    f32 = jnp.float32
    bsz, seq = q.shape[:2]
    n_chunks = -(-seq // GLA_CHUNK)
    pad = n_chunks * GLA_CHUNK - seq

    def prep(t):
        t = jnp.pad(t.astype(f32), ((0, 0), (0, pad), (0, 0), (0, 0)))
        return t.reshape(bsz, n_chunks, GLA_CHUNK, GLA_HEADS, t.shape[-1]).transpose(1, 0, 3, 2, 4)

    qc = prep(q.astype(f32) * (GLA_DK ** -0.5))
    kc, vc, gc = prep(k), prep(v), prep(log_a)
    causal = jnp.tril(jnp.ones((GLA_CHUNK, GLA_CHUNK), dtype=bool))

    def step(state, inp):
        qi, ki, vi, gi = inp
        b = jnp.cumsum(gi, axis=2)
        b_last = b[:, :, -1:, :]
        qt = qi * jnp.exp(b)
        kt = ki * jnp.exp(-b)
        att = jnp.where(causal, jnp.einsum('bhik,bhjk->bhij', qt, kt), 0.0)
        o = jnp.einsum('bhij,bhjv->bhiv', att, vi) + jnp.einsum('bhik,bhkv->bhiv', qt, state)
        state = (jnp.exp(b_last[:, :, 0, :, None]) * state
                 + jnp.einsum('bhjk,bhjv->bhkv', ki * jnp.exp(b_last - b), vi))
        return state, o

    s_fin, o = lax.scan(step, s0.astype(f32), (qc, kc, vc, gc))
    o = o.transpose(1, 0, 3, 2, 4).reshape(bsz, n_chunks * GLA_CHUNK, GLA_HEADS, GLA_DV)[:, :seq]
    return o, s_fin


def _grouped_expert_ffn(h, expert, gate, w_gate, w_up, w_down):
    n_tok, dm = h.shape
    m = n_tok * TOP_K
    flat_e = expert.reshape(m).astype(jnp.int32)
    flat_t = jnp.repeat(jnp.arange(n_tok, dtype=jnp.int32), TOP_K)
    flat_w = gate.reshape(m).astype(jnp.float32)
    counts = jnp.bincount(flat_e, length=N_EXPERTS).astype(jnp.int32)
    padded = (counts + MOE_BLOCK - 1) // MOE_BLOCK * MOE_BLOCK
    pad_end = jnp.cumsum(padded)
    pad_start = pad_end - padded
    raw_start = jnp.cumsum(counts) - counts
    order = jnp.argsort(flat_e, stable=True)
    se = flat_e[order]
    dest = pad_start[se] + jnp.arange(m, dtype=jnp.int32) - raw_start[se]
    n_blocks = (m + N_EXPERTS * (MOE_BLOCK - 1) + MOE_BLOCK - 1) // MOE_BLOCK
    n_slots = n_blocks * MOE_BLOCK
    slot_tok = jnp.full((n_slots,), n_tok, jnp.int32).at[dest].set(flat_t[order])
    slot_w = jnp.zeros((n_slots,), jnp.float32).at[dest].set(flat_w[order])
    blk_start = jnp.arange(n_blocks, dtype=jnp.int32) * MOE_BLOCK
    blk_e = jnp.minimum(jnp.searchsorted(pad_end, blk_start, side='right'), N_EXPERTS - 1)
    h_pad = jnp.concatenate([h, jnp.zeros((1, dm), h.dtype)], axis=0)
    xb = h_pad[slot_tok].reshape(n_blocks, MOE_BLOCK, dm)

    def expert_block(args):
        xblk, e = args
        return (jax.nn.silu(xblk @ w_gate[e]) * (xblk @ w_up[e])) @ w_down[e]

    yb = lax.map(expert_block, (xb, blk_e)).reshape(n_slots, dm)
    out = jax.ops.segment_sum(yb.astype(jnp.float32) * slot_w[:, None], slot_tok, num_segments=n_tok + 1)
    return out[:n_tok]


def hier_moe(h, grp_w, grp_b, exp_w, exp_b, w_gate, w_up, w_down):
    f32 = jnp.float32
    hf = h.astype(f32)
    g_prob = jax.nn.softmax(hf @ grp_w.astype(f32) + grp_b.astype(f32), axis=-1)
    g_top, g_idx = lax.top_k(g_prob, 1)
    e_all = jnp.einsum('nd,dge->nge', hf, exp_w.astype(f32)) + exp_b.astype(f32)
    e_logits = jnp.einsum('nge,ng->ne', e_all, jax.nn.one_hot(g_idx[:, 0], MOE_GROUPS, dtype=f32))
    e_top, e_idx = lax.top_k(e_logits, TOP_K)
    gate = jax.nn.softmax(e_top, axis=-1) * g_top
    expert = g_idx * EXPERTS_PER_GROUP + e_idx
    return _grouped_expert_ffn(h, expert, gate, w_gate, w_up, w_down)


def trunk_layer(x, buf_k, buf_v, ssm_re0, ssm_im0, gla_s0, lw):
    (norm_mix_g, w_in, ssm_lam_re, ssm_lam_im, ssm_log_dt, ssm_b_re, ssm_b_im, ssm_c_re, ssm_c_im,
     ssm_d, glu_w, glu_b, attn_sinks, gla_gate_w2, gla_gate_b, gla_norm_g, w_out, norm_ffn_g,
     router_grp_w, router_grp_b, router_exp_w, router_exp_b, moe_w_gate, moe_w_up, moe_w_down) = lw
    f32 = jnp.float32
    bsz, seq, _ = x.shape
    h = rmsnorm(x, norm_mix_g)
    u, aq, ak, av, gq, gk, gv, gr, glr = _split_cols(h @ w_in)
    ya, ssm_re, ssm_im = s5_mixer(u, ssm_re0, ssm_im0, ssm_lam_re, ssm_lam_im, ssm_log_dt,
                                  ssm_b_re, ssm_b_im, ssm_c_re, ssm_c_im, ssm_d)
    ya = jax.nn.gelu(ya)
    a_out = ya * jax.nn.sigmoid(ya @ glu_w.astype(f32) + glu_b.astype(f32))
    aq = aq.reshape(bsz, seq, ATT_HEADS, HEAD_DIM)
    ak = ak.reshape(bsz, seq, KV_HEADS, HEAD_DIM)
    av = av.reshape(bsz, seq, KV_HEADS, HEAD_DIM)
    if buf_k is None:
        b_out = swa_prompt(aq, ak, av, attn_sinks)
        keep = min(WINDOW, seq)
        new_k, new_v = ak[:, seq - keep:], av[:, seq - keep:]
    else:
        b_out, new_k, new_v = swa_sample(aq, ak, av, buf_k, buf_v, attn_sinks)
    log_a = jax.nn.log_sigmoid(glr.astype(f32) @ gla_gate_w2.astype(f32) + gla_gate_b.astype(f32)) / GLA_TAU
    o, gla_s = gla_chunked(gq.reshape(bsz, seq, GLA_HEADS, GLA_DK), gk.reshape(bsz, seq, GLA_HEADS, GLA_DK),
                           gv.reshape(bsz, seq, GLA_HEADS, GLA_DV), log_a.reshape(bsz, seq, GLA_HEADS, GLA_DK), gla_s0)
    o = o * lax.rsqrt(jnp.mean(o * o, axis=-1, keepdims=True) + RMS_EPS) * gla_norm_g.astype(f32)
    c_out = o.reshape(bsz, seq, GLA_W) * jax.nn.silu(gr.astype(f32))
    mixed = jnp.concatenate([a_out, b_out, c_out], axis=-1).astype(x.dtype)
    x = x + mixed @ w_out
    hn = rmsnorm(x, norm_ffn_g).reshape(bsz * seq, D_MODEL)
    ff = hier_moe(hn, router_grp_w, router_grp_b, router_exp_w, router_exp_b, moe_w_gate, moe_w_up, moe_w_down)
    x = x + ff.reshape(bsz, seq, D_MODEL).astype(x.dtype)
    return x, new_k, new_v, ssm_re, ssm_im, gla_s


def setup_inputs(seed: int = 0) -> dict:
    key = jax.random.key(seed)
    ks = jax.random.split(key, 40)
    f32 = jnp.float32

    def nrm(k, shape, scale):
        return jax.random.normal(k, shape, f32) * scale

    wb = min(WINDOW, PAST_LEN)
    lam_im_base = math.pi * jnp.arange(SSM_STATE, dtype=f32)
    return {
        'x_prompt': nrm(ks[0], (BATCH, SEQ, D_MODEL), 1.0),
        'x_sample': nrm(ks[1], (DEC_BATCH, DEC_SEQ, D_MODEL), 1.0),
        'cache_win_k': nrm(ks[2], (DEPTH, DEC_BATCH, wb, KV_HEADS, HEAD_DIM), 1.0),
        'cache_win_v': nrm(ks[3], (DEPTH, DEC_BATCH, wb, KV_HEADS, HEAD_DIM), 1.0),
        'state_ssm_re': nrm(ks[4], (DEPTH, DEC_BATCH, SSM_GROUPS, SSM_STATE), 0.1),
        'state_ssm_im': nrm(ks[5], (DEPTH, DEC_BATCH, SSM_GROUPS, SSM_STATE), 0.1),
        'state_gla': nrm(ks[6], (DEPTH, DEC_BATCH, GLA_HEADS, GLA_DK, GLA_DV), 1.0),
        'norm_mix_g': 1.0 + nrm(ks[7], (DEPTH, D_MODEL), 0.02),
        'w_in': nrm(ks[8], (DEPTH, D_MODEL, IN_COLS), D_MODEL ** -0.5),
        'ssm_lam_re': -0.5 + nrm(ks[9], (DEPTH, SSM_GROUPS, SSM_STATE), 0.01),
        'ssm_lam_im': lam_im_base + nrm(ks[10], (DEPTH, SSM_GROUPS, SSM_STATE), 0.01),
        'ssm_log_dt': jax.random.uniform(ks[11], (DEPTH, SSM_GROUPS), f32, math.log(1e-3), math.log(1e-1)),
        'ssm_b_re': nrm(ks[12], (DEPTH, SSM_GROUPS, SSM_STATE, SSM_GROUP), (2.0 * SSM_GROUP) ** -0.5),
        'ssm_b_im': nrm(ks[13], (DEPTH, SSM_GROUPS, SSM_STATE, SSM_GROUP), (2.0 * SSM_GROUP) ** -0.5),
        'ssm_c_re': nrm(ks[14], (DEPTH, SSM_GROUPS, SSM_GROUP, SSM_STATE), (2.0 * SSM_STATE) ** -0.5),
        'ssm_c_im': nrm(ks[15], (DEPTH, SSM_GROUPS, SSM_GROUP, SSM_STATE), (2.0 * SSM_STATE) ** -0.5),
        'ssm_d': nrm(ks[16], (DEPTH, SSM_W), 1.0),
        'glu_w': nrm(ks[17], (DEPTH, SSM_W, SSM_W), SSM_W ** -0.5),
        'glu_b': nrm(ks[18], (DEPTH, SSM_W), 0.02),
        'attn_sinks': nrm(ks[19], (DEPTH, ATT_HEADS), 0.5),
        'gla_gate_w2': nrm(ks[20], (DEPTH, GLA_RANK, GLA_HEADS * GLA_DK), GLA_RANK ** -0.5),
        'gla_gate_b': nrm(ks[21], (DEPTH, GLA_HEADS * GLA_DK), 0.02),
        'gla_norm_g': 1.0 + nrm(ks[22], (DEPTH, GLA_DV), 0.02),
        'w_out': nrm(ks[23], (DEPTH, D_MODEL, D_MODEL), D_MODEL ** -0.5),
        'norm_ffn_g': 1.0 + nrm(ks[24], (DEPTH, D_MODEL), 0.02),
        'router_grp_w': nrm(ks[25], (DEPTH, D_MODEL, MOE_GROUPS), D_MODEL ** -0.5),
        'router_grp_b': nrm(ks[26], (DEPTH, MOE_GROUPS), 0.01),
        'router_exp_w': nrm(ks[27], (DEPTH, D_MODEL, MOE_GROUPS, EXPERTS_PER_GROUP), D_MODEL ** -0.5),
        'router_exp_b': nrm(ks[28], (DEPTH, MOE_GROUPS, EXPERTS_PER_GROUP), 0.01),
        'moe_w_gate': nrm(ks[29], (DEPTH, N_EXPERTS, D_MODEL, D_EXPERT), D_MODEL ** -0.5),
        'moe_w_up': nrm(ks[30], (DEPTH, N_EXPERTS, D_MODEL, D_EXPERT), D_MODEL ** -0.5),
        'moe_w_down': nrm(ks[31], (DEPTH, N_EXPERTS, D_EXPERT, D_MODEL), D_EXPERT ** -0.5),
        'final_norm_g': 1.0 + nrm(ks[32], (D_MODEL,), 0.02),
    }


def reference(x_prompt, x_sample, cache_win_k, cache_win_v, state_ssm_re, state_ssm_im, state_gla,
              norm_mix_g, w_in, ssm_lam_re, ssm_lam_im, ssm_log_dt, ssm_b_re, ssm_b_im, ssm_c_re, ssm_c_im,
              ssm_d, glu_w, glu_b, attn_sinks, gla_gate_w2, gla_gate_b, gla_norm_g, w_out, norm_ffn_g,
              router_grp_w, router_grp_b, router_exp_w, router_exp_b, moe_w_gate, moe_w_up, moe_w_down,
              final_norm_g):
    f32 = jnp.float32
    bp = x_prompt.shape[0]
    xp, xs = x_prompt, x_sample
    p_k, p_v, p_re, p_im, p_g = [], [], [], [], []
    s_k, s_v, s_re, s_im, s_g = [], [], [], [], []
    for l in range(DEPTH):
        lw = (norm_mix_g[l], w_in[l], ssm_lam_re[l], ssm_lam_im[l], ssm_log_dt[l], ssm_b_re[l], ssm_b_im[l],
              ssm_c_re[l], ssm_c_im[l], ssm_d[l], glu_w[l], glu_b[l], attn_sinks[l], gla_gate_w2[l],
              gla_gate_b[l], gla_norm_g[l], w_out[l], norm_ffn_g[l], router_grp_w[l], router_grp_b[l],
              router_exp_w[l], router_exp_b[l], moe_w_gate[l], moe_w_up[l], moe_w_down[l])
        xp, k_, v_, r_, i_, g_ = trunk_layer(
            xp, None, None,
            jnp.zeros((bp, SSM_GROUPS, SSM_STATE), f32), jnp.zeros((bp, SSM_GROUPS, SSM_STATE), f32),
            jnp.zeros((bp, GLA_HEADS, GLA_DK, GLA_DV), f32), lw)
        p_k.append(k_); p_v.append(v_); p_re.append(r_); p_im.append(i_); p_g.append(g_)
        xs, k_, v_, r_, i_, g_ = trunk_layer(
            xs, cache_win_k[l], cache_win_v[l], state_ssm_re[l], state_ssm_im[l], state_gla[l], lw)
        s_k.append(k_); s_v.append(v_); s_re.append(r_); s_im.append(i_); s_g.append(g_)
    y_prompt = rmsnorm(xp, final_norm_g)
    y_sample = rmsnorm(xs, final_norm_g)
    return (y_prompt, y_sample,
            jnp.stack(p_k), jnp.stack(p_v), jnp.stack(p_re), jnp.stack(p_im), jnp.stack(p_g),
            jnp.stack(s_k), jnp.stack(s_v), jnp.stack(s_re), jnp.stack(s_im), jnp.stack(s_g))
```

```python
import functools
import math

import jax
import jax.numpy as jnp
from jax import lax
from jax.experimental import pallas as pl
from jax.experimental.pallas import tpu as pltpu

F32 = jnp.float32
BF16 = jnp.bfloat16

D_MODEL = 4096
SSM_W = D_MODEL // 4
ATT_W = D_MODEL // 2
GLA_W = D_MODEL // 4
SSM_GROUP = 16
SSM_GROUPS = SSM_W // SSM_GROUP
SSM_STATE = 64
SSM_FLAT = SSM_GROUPS * SSM_STATE
HEAD_DIM = 128
ATT_HEADS = ATT_W // HEAD_DIM
KV_HEADS = ATT_HEADS // 4
KV_REP = ATT_HEADS // KV_HEADS
KV_W = KV_HEADS * HEAD_DIM
WINDOW = 128
GLA_HEADS = 4
GLA_DV = GLA_W // GLA_HEADS
GLA_DK = GLA_DV // 2
GLA_QK_W = GLA_HEADS * GLA_DK
GLA_RANK = 16
GLA_TAU = 16.0
GLA_CHUNK = 64
MOE_GROUPS = 4
EXPERTS_PER_GROUP = 8
N_EXPERTS = MOE_GROUPS * EXPERTS_PER_GROUP
TOP_K = 2
D_EXPERT = D_MODEL // 4
RMS_EPS = 1e-6
NEG_BIG = -1e30

COL_U = 0
COL_AQ = COL_U + SSM_W
COL_AK = COL_AQ + ATT_W
COL_AV = COL_AK + KV_W
COL_GQ = COL_AV + KV_W
COL_GK = COL_GQ + GLA_QK_W
COL_GV = COL_GK + GLA_QK_W
COL_GR = COL_GV + GLA_W
COL_GLR = COL_GR + GLA_W
IN_COLS = COL_GLR + GLA_RANK

LANES = 128
SUBLANES = 8
MIB = 1024 * 1024

MOE_ROW_BLOCK = 128
MOE_SUPER_BLOCKS = 4
MOE_F_CHUNK = 256
MOE_F_STEPS = D_EXPERT // MOE_F_CHUNK
ROUTER_COLS = LANES


def _cparams(semantics, vmem_mib):
    return pltpu.CompilerParams(dimension_semantics=semantics,
                                vmem_limit_bytes=vmem_mib * MIB)


def _sigmoid(x):
    return 1.0 / (1.0 + jnp.exp(-x))


def _gelu_tanh(x):
    c = math.sqrt(2.0 / math.pi)
    return 0.5 * x * (1.0 + jnp.tanh(c * (x + 0.044715 * (x * x * x))))


def _rms_scale(x):
    return x * lax.rsqrt(jnp.mean(x * x, axis=-1, keepdims=True) + RMS_EPS)


def _dot(a, b):
    return jnp.dot(a, b, preferred_element_type=F32)


def _dot_nt(a, b):
    return lax.dot_general(a, b, (((1,), (1,)), ((), ())), preferred_element_type=F32)


def _dot_tn(a, b):
    return lax.dot_general(a, b, (((0,), (0,)), ((), ())), preferred_element_type=F32)


def _norm_inproj_kernel(x_ref, g_ref, w_ref, o_ref, h_scr):
    @pl.when(pl.program_id(1) == 0)
    def _():
        h_scr[...] = (_rms_scale(x_ref[...]) * g_ref[...]).astype(BF16)

    half = h_scr.shape[1] // 2
    o_ref[...] = _dot(h_scr[:, :half], w_ref[:half, :]) + _dot(h_scr[:, half:], w_ref[half:, :])


def _norm_inproj(x, g, w_bf16, tm, tn):
    n, d = x.shape
    cols = w_bf16.shape[1]
    return pl.pallas_call(
        _norm_inproj_kernel,
        out_shape=jax.ShapeDtypeStruct((n, cols), F32),
        grid=(n // tm, pl.cdiv(cols, tn)),
        in_specs=[pl.BlockSpec((tm, d), lambda i, j: (i, 0)),
                  pl.BlockSpec((1, d), lambda i, j: (0, 0)),
                  pl.BlockSpec((d, tn), lambda i, j: (0, j))],
        out_specs=pl.BlockSpec((tm, tn), lambda i, j: (i, j)),
        scratch_shapes=[pltpu.VMEM((tm, d), BF16)],
        compiler_params=_cparams(("parallel", "arbitrary"), 48),
        name="norm_inproj",
    )(x, g.reshape(1, d), w_bf16)


def _outproj_kernel(x_ref, a_ref, b_ref, c_ref, w_ref, o_ref):
    acc = _dot(a_ref[...], w_ref[0:SSM_W, :])
    acc += _dot(b_ref[...], w_ref[SSM_W:SSM_W + ATT_W, :])
    acc += _dot(c_ref[...], w_ref[SSM_W + ATT_W:D_MODEL, :])
    o_ref[...] = x_ref[...] + acc


def _outproj(x, a, b, c, w_bf16, tm, tn):
    n, d = x.shape
    return pl.pallas_call(
        _outproj_kernel,
        out_shape=jax.ShapeDtypeStruct((n, d), F32),
        grid=(n // tm, d // tn),
        in_specs=[pl.BlockSpec((tm, tn), lambda i, j: (i, j)),
                  pl.BlockSpec((tm, SSM_W), lambda i, j: (i, 0)),
                  pl.BlockSpec((tm, ATT_W), lambda i, j: (i, 0)),
                  pl.BlockSpec((tm, GLA_W), lambda i, j: (i, 0)),
                  pl.BlockSpec((d, tn), lambda i, j: (0, j))],
        out_specs=pl.BlockSpec((tm, tn), lambda i, j: (i, j)),
        compiler_params=_cparams(("parallel", "arbitrary"), 48),
        name="outproj",
    )(x, a, b, c, w_bf16)


S5_COL_CHUNKS = 4
S5_U_CHUNK = SSM_W // S5_COL_CHUNKS
S5_H_CHUNK = SSM_FLAT // S5_COL_CHUNKS
S5_SCAN_LANES = 512


def _s5_discretize(lam_re, lam_im, log_dt, b_re, b_im):
    lr, li = lam_re.astype(F32), lam_im.astype(F32)
    dt = jnp.exp(log_dt.astype(F32))[:, None]
    mag = jnp.exp(lr * dt)
    ar, ai = mag * jnp.cos(li * dt), mag * jnp.sin(li * dt)
    den = lr * lr + li * li
    zr, zi = ar - 1.0, ai
    cr = (zr * lr + zi * li) / den
    ci = (zi * lr - zr * li) / den
    br, bi = b_re.astype(F32), b_im.astype(F32)
    bbr = cr[..., None] * br - ci[..., None] * bi
    bbi = cr[..., None] * bi + ci[..., None] * br
    return ar, ai, bbr, bbi


def _s5_block_diag_in(bb):
    gpc = SSM_GROUPS // S5_COL_CHUNKS
    b4 = bb.reshape(S5_COL_CHUNKS, gpc, SSM_STATE, SSM_GROUP)
    eye = jnp.eye(gpc, dtype=F32)
    m = jnp.einsum('jgpc,gh->jgchp', b4, eye)
    return m.reshape(S5_COL_CHUNKS, S5_U_CHUNK, S5_H_CHUNK).astype(BF16)


def _s5_block_diag_out(c):
    gpc = SSM_GROUPS // S5_COL_CHUNKS
    c4 = c.astype(F32).reshape(S5_COL_CHUNKS, gpc, SSM_GROUP, SSM_STATE)
    eye = jnp.eye(gpc, dtype=F32)
    m = jnp.einsum('jgcp,gh->jgphc', c4, eye)
    return m.reshape(S5_COL_CHUNKS, S5_H_CHUNK, S5_U_CHUNK).astype(BF16)


def _cmul(ar, ai, br, bi):
    return ar * br - ai * bi, ar * bi + ai * br


def _s5_scan_consts(ar, ai):
    ar, ai = ar.reshape(1, SSM_FLAT), ai.reshape(1, SSM_FLAT)
    pr, pi = [ar], [ai]
    for _ in range(SUBLANES - 1):
        nr, ni = _cmul(pr[-1], pi[-1], ar, ai)
        pr.append(nr)
        pi.append(ni)
    row = jnp.arange(SUBLANES, dtype=jnp.int32)[:, None]
    out = []
    for k in (1, 2, 4):
        m = (row >= k).astype(F32)
        out += [m * pr[k - 1], m * pi[k - 1]]
    out += [jnp.concatenate(pr, axis=0), jnp.concatenate(pi, axis=0)]
    return jnp.stack(out)


def _s5_readout(u, hr_of, hi_of, cre_ref, cim_ref, d_ref, gw_ref, gb_ref):
    ys = []
    for j in range(S5_COL_CHUNKS):
        ys.append(_dot(hr_of(j).astype(BF16), cre_ref[j]) - _dot(hi_of(j).astype(BF16), cim_ref[j]))
    y = jnp.concatenate(ys, axis=1) + d_ref[...] * u
    g = _gelu_tanh(y)
    gate = _sigmoid(_dot(g.astype(BF16), gw_ref[...]) + gb_ref[...])
    return (g * gate).astype(BF16)


def _s5_prompt_kernel(u_ref, bre_ref, bim_ref, kc_ref, cre_ref, cim_ref, d_ref, gw_ref, gb_ref,
                      o_ref, sre_ref, sim_ref, xr_scr, xi_scr, cr_scr, ci_scr):
    tc = u_ref.shape[1]

    @pl.when(pl.program_id(1) == 0)
    def _():
        cr_scr[...] = jnp.zeros_like(cr_scr)
        ci_scr[...] = jnp.zeros_like(ci_scr)

    u = u_ref[0]
    ub = u.astype(BF16)
    for j in range(S5_COL_CHUNKS):
        uj = ub[:, j * S5_U_CHUNK:(j + 1) * S5_U_CHUNK]
        xr_scr[:, j * S5_H_CHUNK:(j + 1) * S5_H_CHUNK] = _dot(uj, bre_ref[j])
        xi_scr[:, j * S5_H_CHUNK:(j + 1) * S5_H_CHUNK] = _dot(uj, bim_ref[j])

    for c in range(SSM_FLAT // S5_SCAN_LANES):
        lo, hi = c * S5_SCAN_LANES, (c + 1) * S5_SCAN_LANES

        def body(k, carry, lo=lo, hi=hi):
            cr, ci = carry
            r0 = pl.multiple_of(k * SUBLANES, SUBLANES)
            xr = xr_scr[pl.ds(r0, SUBLANES), lo:hi]
            xi = xi_scr[pl.ds(r0, SUBLANES), lo:hi]
            for s, shift in enumerate((1, 2, 4)):
                mr = kc_ref[2 * s, :, lo:hi]
                mi = kc_ref[2 * s + 1, :, lo:hi]
                sr = pltpu.roll(xr, shift, 0)
                si = pltpu.roll(xi, shift, 0)
                dr, di = _cmul(mr, mi, sr, si)
                xr, xi = xr + dr, xi + di
            dr, di = _cmul(kc_ref[6, :, lo:hi], kc_ref[7, :, lo:hi], cr, ci)
            hr, hi_ = xr + dr, xi + di
            xr_scr[pl.ds(r0, SUBLANES), lo:hi] = hr
            xi_scr[pl.ds(r0, SUBLANES), lo:hi] = hi_
            return hr[SUBLANES - 1:SUBLANES, :], hi_[SUBLANES - 1:SUBLANES, :]

        cr, ci = lax.fori_loop(0, tc // SUBLANES, body, (cr_scr[:, lo:hi], ci_scr[:, lo:hi]))
        cr_scr[:, lo:hi] = cr
        ci_scr[:, lo:hi] = ci

    sre_ref[0] = cr_scr[...]
    sim_ref[0] = ci_scr[...]
    o_ref[0] = _s5_readout(
        u,
        lambda j: xr_scr[:, j * S5_H_CHUNK:(j + 1) * S5_H_CHUNK],
        lambda j: xi_scr[:, j * S5_H_CHUNK:(j + 1) * S5_H_CHUNK],
        cre_ref, cim_ref, d_ref, gw_ref, gb_ref)


def _const_spec(shape):
    nd = len(shape)
    return pl.BlockSpec(shape, lambda *_: (0,) * nd)


def _s5_prompt(z3, s5w, tc):
    bsz, seq, _ = z3.shape
    bre, bim, kc, cre, cim, d, gw, gb = s5w
    return pl.pallas_call(
        _s5_prompt_kernel,
        out_shape=(jax.ShapeDtypeStruct((bsz, seq, SSM_W), BF16),
                   jax.ShapeDtypeStruct((bsz, 1, SSM_FLAT), F32),
                   jax.ShapeDtypeStruct((bsz, 1, SSM_FLAT), F32)),
        grid=(bsz, seq // tc),
        in_specs=[pl.BlockSpec((1, tc, SSM_W), lambda b, i: (b, i, COL_U // SSM_W)),
                  _const_spec(bre.shape), _const_spec(bim.shape), _const_spec(kc.shape),
                  _const_spec(cre.shape), _const_spec(cim.shape), _const_spec(d.shape),
                  _const_spec(gw.shape), _const_spec(gb.shape)],
        out_specs=(pl.BlockSpec((1, tc, SSM_W), lambda b, i: (b, i, 0)),
                   pl.BlockSpec((1, 1, SSM_FLAT), lambda b, i: (b, 0, 0)),
                   pl.BlockSpec((1, 1, SSM_FLAT), lambda b, i: (b, 0, 0))),
        scratch_shapes=[pltpu.VMEM((tc, SSM_FLAT), F32), pltpu.VMEM((tc, SSM_FLAT), F32),
                        pltpu.VMEM((1, SSM_FLAT), F32), pltpu.VMEM((1, SSM_FLAT), F32)],
        compiler_params=_cparams(("parallel", "arbitrary"), 48),
        name="s5_prompt",
    )(z3, bre, bim, kc, cre, cim, d, gw, gb)


def _s5_step_kernel(u_ref, s0r_ref, s0i_ref, bre_ref, bim_ref, kc_ref, cre_ref, cim_ref,
                    d_ref, gw_ref, gb_ref, o_ref, sre_ref, sim_ref):
    u = u_ref[...]
    ub = u.astype(BF16)
    for j in range(S5_COL_CHUNKS):
        lo, hi = j * S5_H_CHUNK, (j + 1) * S5_H_CHUNK
        uj = ub[:, j * S5_U_CHUNK:(j + 1) * S5_U_CHUNK]
        ar = kc_ref[6, 0:1, lo:hi]
        ai = kc_ref[7, 0:1, lo:hi]
        dr, di = _cmul(ar, ai, s0r_ref[:, lo:hi], s0i_ref[:, lo:hi])
        sre_ref[:, lo:hi] = _dot(uj, bre_ref[j]) + dr
        sim_ref[:, lo:hi] = _dot(uj, bim_ref[j]) + di
    o_ref[...] = _s5_readout(
        u,
        lambda j: sre_ref[:, j * S5_H_CHUNK:(j + 1) * S5_H_CHUNK],
        lambda j: sim_ref[:, j * S5_H_CHUNK:(j + 1) * S5_H_CHUNK],
        cre_ref, cim_ref, d_ref, gw_ref, gb_ref)


def _s5_step(u, s0r, s0i, s5w):
    n = u.shape[0]
    args = (u, s0r, s0i) + tuple(s5w)
    return pl.pallas_call(
        _s5_step_kernel,
        out_shape=(jax.ShapeDtypeStruct((n, SSM_W), BF16),
                   jax.ShapeDtypeStruct((n, SSM_FLAT), F32),
                   jax.ShapeDtypeStruct((n, SSM_FLAT), F32)),
        grid=(1,),
        in_specs=[_const_spec(a.shape) for a in args],
        out_specs=(_const_spec((n, SSM_W)), _const_spec((n, SSM_FLAT)), _const_spec((n, SSM_FLAT))),
        compiler_params=_cparams(("arbitrary",), 48),
        name="s5_step",
    )(*args)


def _softmax_sink_pv(s, sink_col, vv):
    m = jnp.maximum(jnp.max(s, axis=1, keepdims=True), sink_col)
    p = jnp.exp(s - m)
    denom = jnp.sum(p, axis=1, keepdims=True) + jnp.exp(sink_col - m)
    return _dot((p / denom).astype(BF16), vv)


def _swa_prompt_kernel(q0_ref, q1_ref, kc_ref, kp_ref, vc_ref, vp_ref, sink_ref, o_ref):
    i = pl.program_id(1)
    rows = KV_REP * WINDOW
    qi = lax.broadcasted_iota(jnp.int32, (rows, 2 * WINDOW), 0) & (WINDOW - 1)
    kj = lax.broadcasted_iota(jnp.int32, (rows, 2 * WINDOW), 1)
    valid = (kj > qi) & (kj <= qi + WINDOW) & ((kj >= WINDOW) | (i > 0))
    scale = HEAD_DIM ** -0.5
    heads_per_q = ATT_HEADS // 2
    for g in range(KV_HEADS):
        cs = slice(g * HEAD_DIM, (g + 1) * HEAD_DIM)
        kk = jnp.concatenate([kp_ref[0, :, cs], kc_ref[0, :, cs]], axis=0).astype(BF16)
        vv = jnp.concatenate([vp_ref[0, :, cs], vc_ref[0, :, cs]], axis=0).astype(BF16)
        qs, sinks = [], []
        for r in range(KV_REP):
            h = g * KV_REP + r
            q_ref = q0_ref if h < heads_per_q else q1_ref
            hc = (h % heads_per_q) * HEAD_DIM
            qs.append(q_ref[0, :, hc:hc + HEAD_DIM])
            sinks.append(jnp.broadcast_to(sink_ref[h:h + 1, 0:1], (WINDOW, 1)))
        q = jnp.concatenate(qs, axis=0).astype(BF16)
        sink_col = jnp.concatenate(sinks, axis=0)
        s = jnp.where(valid, _dot_nt(q, kk) * scale, NEG_BIG)
        o = _softmax_sink_pv(s, sink_col, vv)
        for r in range(KV_REP):
            h = g * KV_REP + r
            o_ref[0, :, h * HEAD_DIM:(h + 1) * HEAD_DIM] = o[r * WINDOW:(r + 1) * WINDOW].astype(BF16)


def _swa_prompt(z3, sinks_b):
    bsz, seq, _ = z3.shape
    half = ATT_W // 2
    cur = lambda col: (lambda b, i: (b, i, col))
    prev = lambda col: (lambda b, i: (b, jnp.maximum(i - 1, 0), col))
    return pl.pallas_call(
        _swa_prompt_kernel,
        out_shape=jax.ShapeDtypeStruct((bsz, seq, ATT_W), BF16),
        grid=(bsz, seq // WINDOW),
        in_specs=[pl.BlockSpec((1, WINDOW, half), cur(COL_AQ // half)),
                  pl.BlockSpec((1, WINDOW, half), cur(COL_AQ // half + 1)),
                  pl.BlockSpec((1, WINDOW, KV_W), cur(COL_AK // KV_W)),
                  pl.BlockSpec((1, WINDOW, KV_W), prev(COL_AK // KV_W)),
                  pl.BlockSpec((1, WINDOW, KV_W), cur(COL_AV // KV_W)),
                  pl.BlockSpec((1, WINDOW, KV_W), prev(COL_AV // KV_W)),
                  _const_spec(sinks_b.shape)],
        out_specs=pl.BlockSpec((1, WINDOW, ATT_W), lambda b, i: (b, i, 0)),
        compiler_params=_cparams(("parallel", "arbitrary"), 32),
        name="swa_prompt",
    )(z3, z3, z3, z3, z3, z3, sinks_b)


def _swa_step_kernel(q_ref, k_ref, v_ref, sink_ref, o_ref):
    q = q_ref[0].astype(BF16)
    sink_col = sink_ref[:, 0:1]
    head = lax.broadcasted_iota(jnp.int32, (ATT_HEADS, HEAD_DIM), 0)
    scale = HEAD_DIM ** -0.5
    out = jnp.zeros((ATT_HEADS, HEAD_DIM), F32)
    for g in range(KV_HEADS):
        cs = slice(g * HEAD_DIM, (g + 1) * HEAD_DIM)
        kk = k_ref[0, :, cs].astype(BF16)
        vv = v_ref[0, :, cs].astype(BF16)
        o = _softmax_sink_pv(_dot_nt(q, kk) * scale, sink_col, vv)
        out = jnp.where((head >= g * KV_REP) & (head < (g + 1) * KV_REP), o, out)
    o_ref[0] = out.astype(BF16)


def _swa_step(q3, win_k, win_v, sinks_b):
    n = q3.shape[0]
    return pl.pallas_call(
        _swa_step_kernel,
        out_shape=jax.ShapeDtypeStruct((n, ATT_HEADS, HEAD_DIM), BF16),
        grid=(n,),
        in_specs=[pl.BlockSpec((1, ATT_HEADS, HEAD_DIM), lambda b: (b, 0, 0)),
                  pl.BlockSpec((1, WINDOW, KV_W), lambda b: (b, 0, 0)),
                  pl.BlockSpec((1, WINDOW, KV_W), lambda b: (b, 0, 0)),
                  _const_spec(sinks_b.shape)],
        out_specs=pl.BlockSpec((1, ATT_HEADS, HEAD_DIM), lambda b: (b, 0, 0)),
        compiler_params=_cparams(("parallel",), 32),
        name="swa_step",
    )(q3, win_k, win_v, sinks_b)


def _log_sigmoid(x):
    return jnp.minimum(x, 0.0) - jnp.log1p(jnp.exp(-jnp.abs(x)))


def _gla_log_decay(glr, w2_ref, gb_ref):
    lane = lax.broadcasted_iota(jnp.int32, glr.shape, 1)
    low = jnp.where(lane < GLA_RANK, glr, 0.0).astype(BF16)
    return _log_sigmoid(_dot(low, w2_ref[...]) + gb_ref[...]) * (1.0 / GLA_TAU)


def _col_to_lanes(row, width):
    n = row.shape[1]
    sq = jnp.broadcast_to(row, (n, n)).T
    return jnp.concatenate([sq] * (width // n), axis=1) if width != n else sq


def _gla_finish(o, gr, ng_ref):
    o = o * lax.rsqrt(jnp.mean(o * o, axis=-1, keepdims=True) + RMS_EPS) * ng_ref[...]
    return o * (gr * _sigmoid(gr))


def _gla_prompt_kernel(q_ref, k_ref, v_ref, r_ref, lr_ref, w2_ref, gb_ref, ng_ref,
                       o_ref, st_ref, s_scr):
    tc = q_ref.shape[1]

    @pl.when(pl.program_id(1) == 0)
    def _():
        s_scr[...] = jnp.zeros_like(s_scr)

    ci = lax.broadcasted_iota(jnp.int32, (GLA_CHUNK, GLA_CHUNK), 0)
    cj = lax.broadcasted_iota(jnp.int32, (GLA_CHUNK, GLA_CHUNK), 1)
    causal = cj <= ci
    tri = causal.astype(BF16)
    qscale = GLA_DK ** -0.5
    for c in range(tc // GLA_CHUNK):
        rs = slice(c * GLA_CHUNK, (c + 1) * GLA_CHUNK)
        log_a = _gla_log_decay(lr_ref[0, rs, :], w2_ref, gb_ref)
        for h in range(GLA_HEADS):
            ks = slice(h * GLA_DK, (h + 1) * GLA_DK)
            vs = slice(h * GLA_DV, (h + 1) * GLA_DV)
            g = log_a[:, ks]
            g_hi = g.astype(BF16)
            g_lo = (g - g_hi.astype(F32)).astype(BF16)
            b = _dot(tri, g_hi) + _dot(tri, g_lo)
            b_last = b[GLA_CHUNK - 1:GLA_CHUNK, :]
            qt = (q_ref[0, rs, ks] * qscale * jnp.exp(b)).astype(BF16)
            kk = k_ref[0, rs, ks]
            kt = (kk * jnp.exp(-b)).astype(BF16)
            kd = (kk * jnp.exp(b_last - b)).astype(BF16)
            vb = v_ref[0, rs, vs].astype(BF16)
            att = jnp.where(causal, _dot_nt(qt, kt), 0.0).astype(BF16)
            s_old = s_scr[h]
            o = _dot(att, vb) + _dot(qt, s_old.astype(BF16))
            s_scr[h] = _col_to_lanes(jnp.exp(b_last), GLA_DV) * s_old + _dot_tn(kd, vb)
            o_ref[0, rs, vs] = _gla_finish(o, r_ref[0, rs, vs], ng_ref).astype(BF16)
    st_ref[0] = s_scr[...]


def _gla_prompt(z3, glaw, tc):
    bsz, seq, _ = z3.shape
    w2p, gb, ng = glaw
    col = lambda c: (lambda b, i: (b, i, c))
    return pl.pallas_call(
        _gla_prompt_kernel,
        out_shape=(jax.ShapeDtypeStruct((bsz, seq, GLA_W), BF16),
                   jax.ShapeDtypeStruct((bsz, GLA_HEADS, GLA_DK, GLA_DV), F32)),
        grid=(bsz, seq // tc),
        in_specs=[pl.BlockSpec((1, tc, GLA_QK_W), col(COL_GQ // GLA_QK_W)),
                  pl.BlockSpec((1, tc, GLA_QK_W), col(COL_GK // GLA_QK_W)),
                  pl.BlockSpec((1, tc, GLA_W), col(COL_GV // GLA_W)),
                  pl.BlockSpec((1, tc, GLA_W), col(COL_GR // GLA_W)),
                  pl.BlockSpec((1, tc, LANES), col(COL_GLR // LANES)),
                  _const_spec(w2p.shape), _const_spec(gb.shape), _const_spec(ng.shape)],
        out_specs=(pl.BlockSpec((1, tc, GLA_W), lambda b, i: (b, i, 0)),
                   pl.BlockSpec((1, GLA_HEADS, GLA_DK, GLA_DV), lambda b, i: (b, 0, 0, 0))),
        scratch_shapes=[pltpu.VMEM((GLA_HEADS, GLA_DK, GLA_DV), F32)],
        compiler_params=_cparams(("parallel", "arbitrary"), 32),
        name="gla_prompt",
    )(z3, z3, z3, z3, z3, w2p, gb, ng)


def _gla_step_kernel(q_ref, k_ref, v_ref, r_ref, lr_ref, s0_ref, w2_ref, gb_ref, ng_ref,
                     o_ref, st_ref):
    rows = (SUBLANES, 1)
    log_a = _gla_log_decay(jnp.tile(lr_ref[0], rows), w2_ref, gb_ref)[0:1, :]
    qscale = GLA_DK ** -0.5
    rnd = lambda t: t.astype(BF16).astype(F32)
    for h in range(GLA_HEADS):
        ks = slice(h * GLA_DK, (h + 1) * GLA_DK)
        vs = slice(h * GLA_DV, (h + 1) * GLA_DV)
        g0 = log_a[:, ks]
        k = k_ref[0, :, ks]
        v = rnd(v_ref[0, :, vs])
        s0 = s0_ref[0, h]
        qt = q_ref[0, :, ks] * qscale * jnp.exp(g0)
        kt = k * jnp.exp(-g0)
        att = jnp.sum(rnd(qt) * rnd(kt), axis=1, keepdims=True)
        o = rnd(att) * v + _dot(jnp.tile(qt, rows).astype(BF16), s0.astype(BF16))[0:1, :]
        st_ref[0, h] = (_col_to_lanes(jnp.exp(g0), GLA_DV) * s0
                        + _col_to_lanes(rnd(k), GLA_DV) * v)
        o_ref[0, :, vs] = _gla_finish(o, r_ref[0, :, vs], ng_ref).astype(BF16)


def _gla_step(zs3, s0, glaw):
    n = zs3.shape[0]
    w2p, gb, ng = glaw
    col = lambda c: (lambda b: (b, 0, c))
    return pl.pallas_call(
        _gla_step_kernel,
        out_shape=(jax.ShapeDtypeStruct((n, 1, GLA_W), BF16),
                   jax.ShapeDtypeStruct((n, GLA_HEADS, GLA_DK, GLA_DV), F32)),
        grid=(n,),
        in_specs=[pl.BlockSpec((1, 1, GLA_QK_W), col(COL_GQ // GLA_QK_W)),
                  pl.BlockSpec((1, 1, GLA_QK_W), col(COL_GK // GLA_QK_W)),
                  pl.BlockSpec((1, 1, GLA_W), col(COL_GV // GLA_W)),
                  pl.BlockSpec((1, 1, GLA_W), col(COL_GR // GLA_W)),
                  pl.BlockSpec((1, 1, LANES), col(COL_GLR // LANES)),
                  pl.BlockSpec((1, GLA_HEADS, GLA_DK, GLA_DV), lambda b: (b, 0, 0, 0)),
                  _const_spec(w2p.shape), _const_spec(gb.shape), _const_spec(ng.shape)],
        out_specs=(pl.BlockSpec((1, 1, GLA_W), lambda b: (b, 0, 0)),
                   pl.BlockSpec((1, GLA_HEADS, GLA_DK, GLA_DV), lambda b: (b, 0, 0, 0))),
        compiler_params=_cparams(("parallel",), 32),
        name="gla_step",
    )(zs3, zs3, zs3, zs3, zs3, s0, w2p, gb, ng)


def _router_kernel(x_ref, g_ref, w_ref, b_ref, h_ref, info_ref):
    h = _rms_scale(x_ref[...]) * g_ref[...]
    h_ref[...] = h
    logits = _dot(h.astype(BF16), w_ref[...]) + b_ref[...]
    lane = lax.broadcasted_iota(jnp.int32, logits.shape, 1).astype(F32)
    last = float(ROUTER_COLS - 1)

    def first_argmax(v, vmax):
        return jnp.min(jnp.where(v == vmax, lane, last), axis=1, keepdims=True)

    is_grp = lane < MOE_GROUPS
    gl = jnp.where(is_grp, logits, NEG_BIG)
    gmax = jnp.max(gl, axis=1, keepdims=True)
    gsum = jnp.sum(jnp.where(is_grp, jnp.exp(gl - gmax), 0.0), axis=1, keepdims=True)
    g_top = 1.0 / gsum
    gidx = first_argmax(gl, gmax)
    lo = MOE_GROUPS + EXPERTS_PER_GROUP * gidx
    el = jnp.where((lane >= lo) & (lane < lo + EXPERTS_PER_GROUP),
                   logits.astype(BF16).astype(F32), NEG_BIG)
    e1 = jnp.max(el, axis=1, keepdims=True)
    i1 = first_argmax(el, e1)
    el2 = jnp.where(lane == i1, NEG_BIG, el)
    e2 = jnp.max(el2, axis=1, keepdims=True)
    i2 = first_argmax(el2, e2)
    t = jnp.exp(e2 - e1)
    p1 = 1.0 / (1.0 + t)
    p2 = t / (1.0 + t)
    info = jnp.where(lane == 0, i1 - MOE_GROUPS,
                     jnp.where(lane == 1, i2 - MOE_GROUPS,
                               jnp.where(lane == 2, p1 * g_top,
                                         jnp.where(lane == 3, p2 * g_top, 0.0))))
    info_ref[...] = info


def _router(x, g, w, bias, tm):
    n, d = x.shape
    return pl.pallas_call(
        _router_kernel,
        out_shape=(jax.ShapeDtypeStruct((n, d), F32),
                   jax.ShapeDtypeStruct((n, ROUTER_COLS), F32)),
        grid=(n // tm,),
        in_specs=[pl.BlockSpec((tm, d), lambda i: (i, 0)),
                  _const_spec((1, d)), _const_spec(w.shape), _const_spec(bias.shape)],
        out_specs=(pl.BlockSpec((tm, d), lambda i: (i, 0)),
                   pl.BlockSpec((tm, ROUTER_COLS), lambda i: (i, 0))),
        compiler_params=_cparams(("parallel",), 40),
        name="router",
    )(x, g.reshape(1, d), w, bias)


GATHER_STEP_ROWS = MOE_ROW_BLOCK


def _gather_rows_kernel(idx_ref, src_a_ref, src_b_ref, dst_ref, sem):
    s = pl.program_id(0)
    n_a = src_a_ref.shape[0]

    def copy(src_ref, row, r):
        return pltpu.make_async_copy(src_ref.at[pl.ds(row, 1)], dst_ref.at[pl.ds(r, 1)], sem)

    def start(j, _):
        r = s * GATHER_STEP_ROWS + j
        row = idx_ref[r]

        @pl.when(row < n_a)
        def _():
            copy(src_a_ref, row, r).start()

        @pl.when(row >= n_a)
        def _():
            copy(src_b_ref, row - n_a, r).start()
        return 0

    def wait_step(step):
        def wait(j, _):
            copy(src_a_ref, 0, step * GATHER_STEP_ROWS + j).wait()
            return 0
        lax.fori_loop(0, GATHER_STEP_ROWS, wait, 0)

    lax.fori_loop(0, GATHER_STEP_ROWS, start, 0)

    @pl.when(s > 0)
    def _():
        wait_step(s - 1)

    @pl.when(s == pl.num_programs(0) - 1)
    def _():
        wait_step(s)


def _gather_rows(idx, src_a, src_b, n_rows):
    assert n_rows % GATHER_STEP_ROWS == 0
    return pl.pallas_call(
        _gather_rows_kernel,
        out_shape=jax.ShapeDtypeStruct((n_rows, src_a.shape[1]), src_a.dtype),
        grid_spec=pltpu.PrefetchScalarGridSpec(
            num_scalar_prefetch=1, grid=(n_rows // GATHER_STEP_ROWS,),
            in_specs=[pl.BlockSpec(memory_space=pl.ANY), pl.BlockSpec(memory_space=pl.ANY)],
            out_specs=pl.BlockSpec(memory_space=pl.ANY),
            scratch_shapes=[pltpu.SemaphoreType.DMA]),
        compiler_params=_cparams(("arbitrary",), 16),
        name="gather_rows",
    )(idx, src_a, src_b)


def _moe_kernel(e_ref, first_ref, nblk_ref, xb_ref, wg_ref, wu_ref, wd_ref, yb_ref,
                stage_scr, x_scr, acc_scr, sem_in, sem_out):
    s = pl.program_id(0)
    f = pl.program_id(1)
    nblk = nblk_ref[s]
    row0 = first_ref[s] * MOE_ROW_BLOCK

    def in_copy(j):
        slot = j % 2
        return pltpu.make_async_copy(xb_ref.at[pl.ds(row0 + j * MOE_ROW_BLOCK, MOE_ROW_BLOCK)],
                                     stage_scr.at[slot], sem_in.at[slot])

    def out_copy(j):
        rows = pl.ds(j * MOE_ROW_BLOCK, MOE_ROW_BLOCK)
        return pltpu.make_async_copy(acc_scr.at[rows],
                                     yb_ref.at[pl.ds(row0 + j * MOE_ROW_BLOCK, MOE_ROW_BLOCK)], sem_out)

    def for_blocks(fn):
        def body(j, _):
            fn(j)
            return 0
        lax.fori_loop(0, nblk, body, 0)

    def load_block(j):
        @pl.when(j + 1 < nblk)
        def _():
            in_copy(j + 1).start()

        in_copy(j).wait()
        rows = pl.ds(pl.multiple_of(j * MOE_ROW_BLOCK, MOE_ROW_BLOCK), MOE_ROW_BLOCK)
        x_scr[rows, :] = stage_scr[j % 2].astype(BF16)

    @pl.when((f == 0) & (nblk > 0))
    def _():
        in_copy(0).start()
        for_blocks(load_block)

    wg = wg_ref[0].astype(BF16)
    wu = wu_ref[0].astype(BF16)
    wd = wd_ref[0].astype(BF16)
    for n in range(1, MOE_SUPER_BLOCKS + 1):
        @pl.when(nblk == n)
        def _(n=n):
            rows = slice(0, n * MOE_ROW_BLOCK)
            x = x_scr[rows, :]
            a = _dot(x, wg)
            hmid = (a * _sigmoid(a)) * _dot(x, wu)
            y = _dot(hmid.astype(BF16), wd)

            @pl.when(f == 0)
            def _():
                acc_scr[rows, :] = y

            @pl.when(f > 0)
            def _():
                acc_scr[rows, :] += y

    @pl.when(f == MOE_F_STEPS - 1)
    def _():
        for_blocks(lambda j: out_copy(j).start())
        for_blocks(lambda j: out_copy(j).wait())

    n_super = pl.num_programs(0)

    @pl.when((s == n_super - 1) & (f == MOE_F_STEPS - 1))
    def _():
        zero_rows = pl.ds(0, MOE_ROW_BLOCK)
        acc_scr[zero_rows, :] = jnp.zeros((MOE_ROW_BLOCK, acc_scr.shape[1]), F32)

        def tail_copy(j):
            return pltpu.make_async_copy(acc_scr.at[zero_rows],
                                         yb_ref.at[pl.ds(j * MOE_ROW_BLOCK, MOE_ROW_BLOCK)], sem_out)

        def tail(fn):
            def body(j, _):
                fn(j)
                return 0
            lax.fori_loop(first_ref[n_super], yb_ref.shape[0] // MOE_ROW_BLOCK, body, 0)

        tail(lambda j: tail_copy(j).start())
        tail(lambda j: tail_copy(j).wait())


def _moe_ffn(sb_e, sb_first, sb_nblk, xb, w_gate, w_up, w_down):
    n_slots, d = xb.shape
    n_super = sb_e.shape[0]
    rows = MOE_SUPER_BLOCKS * MOE_ROW_BLOCK

    def f_eff(s, f, nb):
        return jnp.where(nb[s] > 0, f, MOE_F_STEPS - 1)

    return pl.pallas_call(
        _moe_kernel,
        out_shape=jax.ShapeDtypeStruct((n_slots, d), F32),
        grid_spec=pltpu.PrefetchScalarGridSpec(
            num_scalar_prefetch=3, grid=(n_super, MOE_F_STEPS),
            in_specs=[pl.BlockSpec(memory_space=pl.ANY),
                      pl.BlockSpec((1, d, MOE_F_CHUNK), lambda s, f, e, a, nb: (e[s], 0, f_eff(s, f, nb))),
                      pl.BlockSpec((1, d, MOE_F_CHUNK), lambda s, f, e, a, nb: (e[s], 0, f_eff(s, f, nb))),
                      pl.BlockSpec((1, MOE_F_CHUNK, d), lambda s, f, e, a, nb: (e[s], f_eff(s, f, nb), 0))],
            out_specs=pl.BlockSpec(memory_space=pl.ANY),
            scratch_shapes=[pltpu.VMEM((2, MOE_ROW_BLOCK, d), F32), pltpu.VMEM((rows, d), BF16),
                            pltpu.VMEM((rows, d), F32),
                            pltpu.SemaphoreType.DMA((2,)), pltpu.SemaphoreType.DMA]),
        compiler_params=_cparams(("arbitrary", "arbitrary"), 58),
        name="moe_ffn",
    )(sb_e, sb_first, sb_nblk, xb, w_gate, w_up, w_down)


def _combine_kernel(slot_ref, x_ref, info_ref, yb_ref, *rest, final_norm):
    if final_norm:
        g_ref, o_ref, y_scr, sem = rest
    else:
        o_ref, y_scr, sem = rest
    tm = x_ref.shape[0]
    base = pl.program_id(0) * tm * TOP_K

    def copy(r, k):
        return pltpu.make_async_copy(yb_ref.at[pl.ds(slot_ref[base + r * TOP_K + k], 1)],
                                     y_scr.at[k, pl.ds(r, 1)], sem)

    def start(r, _):
        for k in range(TOP_K):
            copy(r, k).start()
        return 0

    def wait(r, _):
        for k in range(TOP_K):
            copy(r, k).wait()
        return 0

    lax.fori_loop(0, tm, start, 0)
    lax.fori_loop(0, tm, wait, 0)
    info = info_ref[...]
    out = x_ref[...] + info[:, 2:3] * y_scr[0] + info[:, 3:4] * y_scr[1]
    if final_norm:
        out = _rms_scale(out) * g_ref[...]
    o_ref[...] = out


def _combine(slots, x, info, yb, tm, final_g=None):
    n, d = x.shape
    final_norm = final_g is not None
    in_specs = [pl.BlockSpec((tm, d), lambda i, s: (i, 0)),
                pl.BlockSpec((tm, ROUTER_COLS), lambda i, s: (i, 0)),
                pl.BlockSpec(memory_space=pl.ANY)]
    args = [slots, x, info, yb]
    if final_norm:
        in_specs.append(pl.BlockSpec((1, d), lambda i, s: (0, 0)))
        args.append(final_g.reshape(1, d))
    return pl.pallas_call(
        functools.partial(_combine_kernel, final_norm=final_norm),
        out_shape=jax.ShapeDtypeStruct((n, d), F32),
        grid_spec=pltpu.PrefetchScalarGridSpec(
            num_scalar_prefetch=1, grid=(n // tm,),
            in_specs=in_specs,
            out_specs=pl.BlockSpec((tm, d), lambda i, s: (i, 0)),
            scratch_shapes=[pltpu.VMEM((TOP_K, tm, d), F32), pltpu.SemaphoreType.DMA]),
        compiler_params=_cparams(("arbitrary",), 32),
        name="combine",
    )(*args)


def _moe_plan(expert, n_tok):
    m = n_tok * TOP_K
    flat_e = expert.reshape(m)
    onehot = (flat_e[:, None] == jnp.arange(N_EXPERTS, dtype=jnp.int32)[None, :]).astype(jnp.int32)
    csum = jnp.cumsum(onehot, axis=0)
    counts = csum[-1]
    rank = jnp.sum(csum * onehot, axis=1) - 1
    blocks = (counts + MOE_ROW_BLOCK - 1) // MOE_ROW_BLOCK
    blk_end = jnp.cumsum(blocks)
    blk_start = blk_end - blocks
    slot = jnp.sum(onehot * blk_start[None, :], axis=1) * MOE_ROW_BLOCK + rank
    n_blocks = (m + N_EXPERTS * (MOE_ROW_BLOCK - 1) + MOE_ROW_BLOCK - 1) // MOE_ROW_BLOCK
    n_slots = n_blocks * MOE_ROW_BLOCK
    tok = jnp.arange(m, dtype=jnp.int32) // TOP_K
    slot_tok = jnp.full((n_slots,), n_tok, jnp.int32).at[slot].set(tok)
    supers = (blocks + MOE_SUPER_BLOCKS - 1) // MOE_SUPER_BLOCKS
    sup_end = jnp.cumsum(supers)
    sup_start = sup_end - supers
    n_super = (n_blocks + (MOE_SUPER_BLOCKS - 1) * N_EXPERTS) // MOE_SUPER_BLOCKS
    sidx = jnp.arange(n_super, dtype=jnp.int32)
    total = sup_end[-1]
    s_clamped = jnp.minimum(sidx, total - 1)
    sb_e = jnp.minimum(jnp.searchsorted(sup_end, s_clamped, side='right'), N_EXPERTS - 1).astype(jnp.int32)
    within = s_clamped - sup_start[sb_e]
    sb_first = blk_start[sb_e] + within * MOE_SUPER_BLOCKS
    sb_nblk = jnp.clip(blocks[sb_e] - within * MOE_SUPER_BLOCKS, 0, MOE_SUPER_BLOCKS)
    sb_nblk = jnp.where(sidx < total, sb_nblk, 0)
    sb_first = jnp.concatenate([sb_first, blk_end[-1:]])
    return slot.astype(jnp.int32), slot_tok, sb_e, sb_first.astype(jnp.int32), sb_nblk.astype(jnp.int32)


def _tile(n, pref):
    return pref if n % pref == 0 else n


def _layer(xp, xs, cache_k, cache_v, ssm_re0, ssm_im0, gla_s0, lw, final_g):
    (norm_mix_g, w_in, ssm_lam_re, ssm_lam_im, ssm_log_dt, ssm_b_re, ssm_b_im, ssm_c_re, ssm_c_im,
     ssm_d, glu_w, glu_b, attn_sinks, gla_gate_w2, gla_gate_b, gla_norm_g, w_out, norm_ffn_g,
     router_grp_w, router_grp_b, router_exp_w, router_exp_b, moe_w_gate, moe_w_up, moe_w_down) = lw
    bsz, seq, _ = xp.shape
    nb = xs.shape[0]
    n_p = bsz * seq
    xp2 = xp.reshape(n_p, D_MODEL)
    xs2 = xs.reshape(nb, D_MODEL)

    w_in_b = w_in.astype(BF16)
    w_out_b = w_out.astype(BF16)
    ar, ai, bbr, bbi = _s5_discretize(ssm_lam_re, ssm_lam_im, ssm_log_dt, ssm_b_re, ssm_b_im)
    s5w = (_s5_block_diag_in(bbr), _s5_block_diag_in(bbi), _s5_scan_consts(ar, ai),
           _s5_block_diag_out(ssm_c_re), _s5_block_diag_out(ssm_c_im),
           ssm_d.astype(F32).reshape(1, SSM_W), glu_w.astype(BF16), glu_b.astype(F32).reshape(1, SSM_W))
    sinks_b = jnp.broadcast_to(attn_sinks.astype(F32)[:, None], (ATT_HEADS, LANES))
    w2p = jnp.zeros((LANES, GLA_QK_W), F32).at[:GLA_RANK].set(gla_gate_w2.astype(F32)).astype(BF16)
    glaw = (w2p, gla_gate_b.astype(F32).reshape(1, GLA_QK_W), gla_norm_g.astype(F32).reshape(1, GLA_DV))
    wr = jnp.concatenate([router_grp_w.astype(F32),
                          router_exp_w.astype(F32).reshape(D_MODEL, N_EXPERTS)], axis=1)
    wr = jnp.pad(wr, ((0, 0), (0, ROUTER_COLS - wr.shape[1]))).astype(BF16)
    br = jnp.concatenate([router_grp_b.astype(F32), router_exp_b.astype(F32).reshape(N_EXPERTS)])
    br = jnp.pad(br, (0, ROUTER_COLS - br.shape[0])).reshape(1, ROUTER_COLS)

    zp = _norm_inproj(xp2, norm_mix_g, w_in_b, _tile(n_p, 512), 512)
    zp3 = zp.reshape(bsz, seq, IN_COLS)
    a_p, p_re, p_im = _s5_prompt(zp3, s5w, _tile(seq, 256))
    b_p = _swa_prompt(zp3, sinks_b)
    c_p, p_gla = _gla_prompt(zp3, glaw, _tile(seq, 256))
    keep = min(WINDOW, seq)
    p_k = zp3[:, seq - keep:, COL_AK:COL_AK + KV_W].reshape(bsz, keep, KV_HEADS, HEAD_DIM)
    p_v = zp3[:, seq - keep:, COL_AV:COL_AV + KV_W].reshape(bsz, keep, KV_HEADS, HEAD_DIM)
    xp2 = _outproj(xp2, a_p.reshape(n_p, SSM_W), b_p.reshape(n_p, ATT_W), c_p.reshape(n_p, GLA_W),
                   w_out_b, _tile(n_p, 512), 512)

    zs = _norm_inproj(xs2, norm_mix_g, w_in_b, nb, 512)
    a_s, s_re, s_im = _s5_step(zs[:, COL_U:COL_U + SSM_W], ssm_re0.reshape(nb, SSM_FLAT),
                               ssm_im0.reshape(nb, SSM_FLAT), s5w)
    wb = cache_k.shape[1]
    new_k = zs[:, COL_AK:COL_AK + KV_W].reshape(nb, 1, KV_HEADS, HEAD_DIM)
    new_v = zs[:, COL_AV:COL_AV + KV_W].reshape(nb, 1, KV_HEADS, HEAD_DIM)
    s_k = jnp.concatenate([cache_k, new_k], axis=1)[:, -wb:]
    s_v = jnp.concatenate([cache_v, new_v], axis=1)[:, -wb:]
    b_s = _swa_step(zs[:, COL_AQ:COL_AQ + ATT_W].reshape(nb, ATT_HEADS, HEAD_DIM),
                    s_k.reshape(nb, wb, KV_W), s_v.reshape(nb, wb, KV_W), sinks_b)
    c_s, s_gla = _gla_step(zs.reshape(nb, 1, IN_COLS), gla_s0, glaw)
    xs2 = _outproj(xs2, a_s, b_s.reshape(nb, ATT_W), c_s.reshape(nb, GLA_W), w_out_b, nb, 512)

    n_tok = n_p + nb
    xs_pad = jnp.concatenate([xs2, jnp.zeros_like(xs2)], axis=0)
    hp, info_p = _router(xp2, norm_ffn_g, wr, br, _tile(n_p, 256))
    hs, info_s = _router(xs_pad, norm_ffn_g, wr, br, 2 * nb)
    info_s = info_s[:nb]
    expert = jnp.concatenate([info_p[:, 0:TOP_K], info_s[:, 0:TOP_K]], axis=0).astype(jnp.int32)
    slot, slot_tok, sb_e, sb_first, sb_nblk = _moe_plan(expert, n_tok)
    xb = _gather_rows(slot_tok, hp, hs, slot_tok.shape[0])
    yb = _moe_ffn(sb_e, sb_first, sb_nblk, xb, moe_w_gate, moe_w_up, moe_w_down)
    xp2 = _combine(slot[:n_p * TOP_K], xp2, info_p, yb, _tile(n_p, 128), final_g)
    xs2 = _combine(slot[n_p * TOP_K:], xs2, info_s, yb, nb, final_g)

    return (xp2.reshape(bsz, seq, D_MODEL), xs2.reshape(nb, 1, D_MODEL),
            (p_k, p_v, p_re.reshape(bsz, SSM_GROUPS, SSM_STATE), p_im.reshape(bsz, SSM_GROUPS, SSM_STATE), p_gla),
            (s_k, s_v, s_re.reshape(nb, SSM_GROUPS, SSM_STATE), s_im.reshape(nb, SSM_GROUPS, SSM_STATE), s_gla))


def kernel(x_prompt, x_sample, cache_win_k, cache_win_v, state_ssm_re, state_ssm_im, state_gla,
           norm_mix_g, w_in, ssm_lam_re, ssm_lam_im, ssm_log_dt, ssm_b_re, ssm_b_im, ssm_c_re, ssm_c_im,
           ssm_d, glu_w, glu_b, attn_sinks, gla_gate_w2, gla_gate_b, gla_norm_g, w_out, norm_ffn_g,
           router_grp_w, router_grp_b, router_exp_w, router_exp_b, moe_w_gate, moe_w_up, moe_w_down,
           final_norm_g):
    depth = w_in.shape[0]
    xp, xs = x_prompt, x_sample
    p_states, s_states = [], []
    for l in range(depth):
        lw = (norm_mix_g[l], w_in[l], ssm_lam_re[l], ssm_lam_im[l], ssm_log_dt[l], ssm_b_re[l], ssm_b_im[l],
              ssm_c_re[l], ssm_c_im[l], ssm_d[l], glu_w[l], glu_b[l], attn_sinks[l], gla_gate_w2[l],
              gla_gate_b[l], gla_norm_g[l], w_out[l], norm_ffn_g[l], router_grp_w[l], router_grp_b[l],
              router_exp_w[l], router_exp_b[l], moe_w_gate[l], moe_w_up[l], moe_w_down[l])
        final_g = final_norm_g if l == depth - 1 else None
        xp, xs, ps, ss = _layer(xp, xs, cache_win_k[l], cache_win_v[l], state_ssm_re[l], state_ssm_im[l],
                                state_gla[l], lw, final_g)
        p_states.append(ps)
        s_states.append(ss)
    stack = lambda states, k: jnp.stack([st[k] for st in states])
    return ((xp, xs) + tuple(stack(p_states, k) for k in range(5))
            + tuple(stack(s_states, k) for k in range(5)))
```
